```python
import jax, jax.numpy as jnp
from jax import lax
import numpy as np

D_MODEL = 2048
BATCH = 8
SEQ = 2048
DEPTH = 1
DEC_BATCH = 128
DEC_SEQ = 8
PAST_LEN = 2048
PAGE_SIZE = 128

HEAD_DIM = 128
N_HEADS = D_MODEL // HEAD_DIM
H_A = N_HEADS // 2
H_B = N_HEADS - H_A
W_A = H_A * HEAD_DIM
W_B = H_B * HEAD_DIM
D_MIX = W_A + W_B
MOBA_BLOCK = 256
MOBA_TOPK = 3
MOBA_Q_BLOCK = 64
SB_Q_BLOCK = 128
PEER_HEADS = 8
PEER_TOPK = 16
PEER_N_KEYS = 128
PEER_N_EXPERTS = PEER_N_KEYS * PEER_N_KEYS
PEER_D_KEY = 256
PEER_TOK_BLOCK = 128
ATTN_SCALE = HEAD_DIM ** -0.5
NEG = -1e30
LN_EPS = 1e-5
RMS_EPS = 1e-6
F32 = jnp.float32

kernel_name = 'hybrid_moba_stickbreaking_peer_step'


def layer_norm(x, g, b):
    xf = x.astype(F32)
    mu = jnp.mean(xf, axis=-1, keepdims=True)
    var = jnp.mean(jnp.square(xf - mu), axis=-1, keepdims=True)
    return ((xf - mu) * lax.rsqrt(var + LN_EPS) * g.astype(F32) + b.astype(F32)).astype(x.dtype)


def rms_norm(x, g):
    xf = x.astype(F32)
    return (xf * lax.rsqrt(jnp.mean(xf * xf, axis=-1, keepdims=True) + RMS_EPS) * g.astype(F32)).astype(x.dtype)


def alibi_slopes(n_heads):
    return jnp.exp2(-8.0 * jnp.arange(1, n_heads + 1, dtype=F32) / n_heads)


def project_qkv(x, w_in):
    B, S, _ = x.shape
    proj = jnp.einsum('bsd,de->bse', x, w_in)
    cuts = [W_A, 2 * W_A, 3 * W_A, 3 * W_A + W_B, 3 * W_A + 2 * W_B]
    parts = jnp.split(proj, cuts, axis=-1)
    return [p.reshape(B, S, -1, HEAD_DIM) for p in parts]


def moba_blocks(k):
    L, nh, hd = k.shape
    nb = -(-L // MOBA_BLOCK)
    k = jnp.pad(k, ((0, nb * MOBA_BLOCK - L), (0, 0), (0, 0)))
    return k.reshape(nb, MOBA_BLOCK, nh, hd).transpose(2, 0, 1, 3)


def moba_attend(q, t, kb, vb, kbar, slopes):
    n_q, n_h, _ = q.shape
    n_blk = kb.shape[1]
    own = t // MOBA_BLOCK
    qf = q.astype(F32)
    gate = jnp.einsum('qhd,hnd->qhn', qf, kbar)
    full_past = jnp.arange(n_blk)[None, None, :] < own[:, None, None]
    gate = jnp.where(full_past, gate, NEG)
    _, sel = lax.top_k(gate, min(MOBA_TOPK, n_blk))
    sel_ok = sel < own[:, None, None]
    own_b = jnp.broadcast_to(own[:, None, None], (n_q, n_h, 1)).astype(sel.dtype)
    blocks = jnp.concatenate([sel, own_b], axis=-1)
    ok = jnp.concatenate([sel_ok, jnp.ones((n_q, n_h, 1), dtype=bool)], axis=-1)
    h_idx = jnp.arange(n_h)[None, :, None]
    k_sel = kb[h_idx, blocks].astype(F32)
    v_sel = vb[h_idx, blocks].astype(F32)
    s_pos = blocks[..., None] * MOBA_BLOCK + jnp.arange(MOBA_BLOCK)
    dist = t[:, None, None, None] - s_pos
    logits = (jnp.einsum('qhd,qhjrd->qhjr', qf, k_sel) * ATTN_SCALE
              - slopes[None, :, None, None] * dist.astype(F32))
    logits = jnp.where(ok[..., None] & (dist >= 0), logits, NEG)
    w = jax.nn.softmax(logits.reshape(n_q, n_h, -1), axis=-1)
    out = jnp.einsum('qhj,qhjd->qhd', w, v_sel.reshape(n_q, n_h, -1, HEAD_DIM))
    return out.astype(q.dtype)


def stick_breaking_attend(q, t, k, v):
    L = k.shape[0]
    z = jnp.einsum('qhd,khd->hqk', q.astype(F32), k.astype(F32)) * ATTN_SCALE
    causal = (jnp.arange(L)[None, :] < t[:, None])[None]
    log_keep = jnp.where(causal, jax.nn.log_sigmoid(-z), 0.0)
    after = lax.cumsum(log_keep, axis=2, reverse=True) - log_keep
    log_a = jnp.where(causal, jax.nn.log_sigmoid(z) + after, NEG)
    out = jnp.einsum('hqk,khd->qhd', jnp.exp(log_a), v.astype(F32))
    return out.astype(q.dtype)


def moba_prompt(q, k, v, slopes):
    _, S, nh, hd = q.shape
    n_qb = S // MOBA_Q_BLOCK

    def one_seq(args):
        qs, ks_, vs = args
        kb = moba_blocks(ks_)
        vb = moba_blocks(vs)
        kbar = jnp.mean(kb.astype(F32), axis=2)
        starts = jnp.arange(n_qb, dtype=jnp.int32) * MOBA_Q_BLOCK

        def one_block(a):
            qq, st = a
            return moba_attend(qq, st + jnp.arange(MOBA_Q_BLOCK, dtype=jnp.int32), kb, vb, kbar, slopes)

        out = lax.map(one_block, (qs.reshape(n_qb, MOBA_Q_BLOCK, nh, hd), starts))
        return out.reshape(S, nh, hd)

    return lax.map(one_seq, (q, k, v))


def sb_prompt(q, k, v):
    B, S, nh, hd = q.shape
    n_qb = S // SB_Q_BLOCK
    q_blocks = q.reshape(B, n_qb, SB_Q_BLOCK, nh, hd).transpose(1, 0, 2, 3, 4)
    starts = jnp.arange(n_qb, dtype=jnp.int32) * SB_Q_BLOCK
    attend = jax.vmap(stick_breaking_attend, in_axes=(0, None, 0, 0))

    def one_block(a):
        qq, st = a
        return attend(qq, st + jnp.arange(SB_Q_BLOCK, dtype=jnp.int32), k, v)

    out = lax.map(one_block, (q_blocks, starts))
    return out.transpose(1, 0, 2, 3, 4).reshape(B, S, nh, hd)


def sample_mix(qa, ka, va, qb, kb, vb, page_table, pool_k_a, pool_v_a, pool_k_b, pool_v_b, slopes):
    T = qa.shape[1]

    def one_seq(args):
        qa_i, ka_i, va_i, qb_i, kb_i, vb_i, pages = args
        P = pages.shape[0] * PAGE_SIZE

        def with_past(pool, new):
            past = pool[pages].reshape(P, new.shape[1], HEAD_DIM).astype(new.dtype)
            return jnp.concatenate([past, new], axis=0)

        ka_all = with_past(pool_k_a, ka_i)
        va_all = with_past(pool_v_a, va_i)
        kb_all = with_past(pool_k_b, kb_i)
        vb_all = with_past(pool_v_b, vb_i)
        t = P + jnp.arange(T, dtype=jnp.int32)
        kblk = moba_blocks(ka_all)
        kbar = jnp.mean(kblk.astype(F32), axis=2)
        oa = moba_attend(qa_i, t, kblk, moba_blocks(va_all), kbar, slopes)
        ob = stick_breaking_attend(qb_i, t, kb_all, vb_all)
        return oa, ob

    return lax.map(one_seq, (qa, ka, va, qb, kb, vb, page_table))


def peer(x, w_query, sub_keys, expert_u, expert_v):
    n, d = x.shape
    n_blk = -(-n // PEER_TOK_BLOCK)
    xp = jnp.pad(x, ((0, n_blk * PEER_TOK_BLOCK - n), (0, 0))).reshape(n_blk, PEER_TOK_BLOCK, d)
    half = PEER_D_KEY // 2

    def one_block(xb):
        q = jnp.einsum('nd,dhk->nhk', xb, w_query).astype(F32)
        s1 = jnp.einsum('nhk,ek->nhe', q[..., :half], sub_keys[0].astype(F32))
        s2 = jnp.einsum('nhk,ek->nhe', q[..., half:], sub_keys[1].astype(F32))
        v1, i1 = lax.top_k(s1, PEER_TOPK)
        v2, i2 = lax.top_k(s2, PEER_TOPK)
        nb_, nh_ = v1.shape[0], v1.shape[1]
        cand = (v1[..., :, None] + v2[..., None, :]).reshape(nb_, nh_, PEER_TOPK * PEER_TOPK)
        cidx = (i1[..., :, None] * PEER_N_KEYS + i2[..., None, :]).reshape(nb_, nh_, PEER_TOPK * PEER_TOPK)
        score, pos = lax.top_k(cand, PEER_TOPK)
        eidx = jnp.take_along_axis(cidx, pos, axis=-1)
        g = jax.nn.softmax(score, axis=-1)
        u = expert_u[eidx]
        act = jax.nn.gelu(jnp.einsum('nd,nhkd->nhk', xb, u).astype(F32))
        return jnp.einsum('nhk,nhkd->nd', (g * act).astype(x.dtype), expert_v[eidx])

    return lax.map(one_block, xp).reshape(-1, d)[:n]


def finish_layer(x, oa, ob, alpha, w_out, norm_a_g, norm_b_g, ln1_g, ln1_b,
                 w_query, sub_keys, expert_u, expert_v, ln2_g, ln2_b):
    B, S, _ = x.shape
    o = jnp.concatenate([rms_norm(oa.reshape(B, S, W_A), norm_a_g),
                         rms_norm(ob.reshape(B, S, W_B), norm_b_g)], axis=-1)
    h = layer_norm(alpha * x + jnp.einsum('bse,ed->bsd', o, w_out), ln1_g, ln1_b)
    f = peer(h.reshape(B * S, D_MODEL), w_query, sub_keys, expert_u, expert_v).reshape(B, S, D_MODEL)
    return layer_norm(alpha * h + f, ln2_g, ln2_b)


def setup_inputs(seed: int = 0) -> dict:
    key = jax.random.key(seed)
    ks = jax.random.split(key, 24)
    nrm = jax.random.normal
    n_pages = PAST_LEN // PAGE_SIZE
    n_used = DEC_BATCH * n_pages
    n_pool = n_used + n_used // 4
    beta = (8.0 * DEPTH) ** -0.25
    std = D_MODEL ** -0.5
    x_prompt = nrm(ks[0], (BATCH, SEQ, D_MODEL), F32)
    x_sample = nrm(ks[1], (DEC_BATCH, DEC_SEQ, D_MODEL), F32)
    cache_k_a = nrm(ks[2], (DEPTH, n_pool, PAGE_SIZE, H_A, HEAD_DIM), F32)
    cache_v_a = nrm(ks[3], (DEPTH, n_pool, PAGE_SIZE, H_A, HEAD_DIM), F32)
    cache_k_b = nrm(ks[4], (DEPTH, n_pool, PAGE_SIZE, H_B, HEAD_DIM), F32)
    cache_v_b = nrm(ks[5], (DEPTH, n_pool, PAGE_SIZE, H_B, HEAD_DIM), F32)
    page_table = jax.random.permutation(ks[6], n_pool)[:n_used].reshape(DEC_BATCH, n_pages).astype(jnp.int32)
    w_qk_a = nrm(ks[7], (DEPTH, D_MODEL, 2 * W_A), F32) * std
    w_v_a = nrm(ks[8], (DEPTH, D_MODEL, W_A), F32) * (std * beta)
    w_qk_b = nrm(ks[9], (DEPTH, D_MODEL, 2 * W_B), F32) * std
    w_v_b = nrm(ks[10], (DEPTH, D_MODEL, W_B), F32) * (std * beta)
    w_in = jnp.concatenate([w_qk_a, w_v_a, w_qk_b, w_v_b], axis=-1)
    w_out = nrm(ks[11], (DEPTH, D_MIX, D_MODEL), F32) * (D_MIX ** -0.5 * beta)
    norm_a_g = 1.0 + 0.02 * nrm(ks[12], (DEPTH, W_A), F32)
    norm_b_g = 1.0 + 0.02 * nrm(ks[13], (DEPTH, W_B), F32)
    ln1_g = 1.0 + 0.02 * nrm(ks[14], (DEPTH, D_MODEL), F32)
    ln1_b = 0.02 * nrm(ks[15], (DEPTH, D_MODEL), F32)
    w_query = nrm(ks[16], (DEPTH, D_MODEL, PEER_HEADS, PEER_D_KEY), F32) * std
    sub_keys = nrm(ks[17], (DEPTH, 2, PEER_N_KEYS, PEER_D_KEY // 2), F32) * (PEER_D_KEY // 2) ** -0.5
    expert_u = nrm(ks[18], (DEPTH, PEER_N_EXPERTS, D_MODEL), F32) * std
    expert_v = nrm(ks[19], (DEPTH, PEER_N_EXPERTS, D_MODEL), F32) * (beta * PEER_HEADS ** -0.5)
    ln2_g = 1.0 + 0.02 * nrm(ks[20], (DEPTH, D_MODEL), F32)
    ln2_b = 0.02 * nrm(ks[21], (DEPTH, D_MODEL), F32)
    return {'x_prompt': x_prompt, 'x_sample': x_sample,
            'cache_k_a': cache_k_a, 'cache_v_a': cache_v_a,
            'cache_k_b': cache_k_b, 'cache_v_b': cache_v_b,
            'page_table': page_table,
            'w_in': w_in, 'w_out': w_out, 'norm_a_g': norm_a_g, 'norm_b_g': norm_b_g,
            'ln1_g': ln1_g, 'ln1_b': ln1_b, 'w_query': w_query, 'sub_keys': sub_keys,
            'expert_u': expert_u, 'expert_v': expert_v, 'ln2_g': ln2_g, 'ln2_b': ln2_b}


def reference(x_prompt, x_sample, cache_k_a, cache_v_a, cache_k_b, cache_v_b, page_table,
              w_in, w_out, norm_a_g, norm_b_g, ln1_g, ln1_b, w_query, sub_keys,
              expert_u, expert_v, ln2_g, ln2_b):
    alpha = (2.0 * DEPTH) ** 0.25
    slopes = alibi_slopes(H_A)
    xp, xs = x_prompt, x_sample
    kap, vap, kbp, vbp = [], [], [], []
    kas, vas, kbs, vbs = [], [], [], []
    for l in range(DEPTH):
        qa, ka, va, qb, kb, vb = project_qkv(xp, w_in[l])
        oa = moba_prompt(qa, ka, va, slopes)
        ob = sb_prompt(qb, kb, vb)
        kap.append(ka); vap.append(va); kbp.append(kb); vbp.append(vb)
        xp = finish_layer(xp, oa, ob, alpha, w_out[l], norm_a_g[l], norm_b_g[l], ln1_g[l], ln1_b[l],
                          w_query[l], sub_keys[l], expert_u[l], expert_v[l], ln2_g[l], ln2_b[l])
        qa, ka, va, qb, kb, vb = project_qkv(xs, w_in[l])
        oa, ob = sample_mix(qa, ka, va, qb, kb, vb, page_table,
                            cache_k_a[l], cache_v_a[l], cache_k_b[l], cache_v_b[l], slopes)
        kas.append(ka); vas.append(va); kbs.append(kb); vbs.append(vb)
        xs = finish_layer(xs, oa, ob, alpha, w_out[l], norm_a_g[l], norm_b_g[l], ln1_g[l], ln1_b[l],
                          w_query[l], sub_keys[l], expert_u[l], expert_v[l], ln2_g[l], ln2_b[l])
    return (xp, xs,
            jnp.stack(kap), jnp.stack(vap), jnp.stack(kbp), jnp.stack(vbp),
            jnp.stack(kas), jnp.stack(vas), jnp.stack(kbs), jnp.stack(vbs))
```

```python
import functools

import jax
import jax.numpy as jnp
from jax import lax
from jax.experimental import pallas as pl
from jax.experimental.pallas import tpu as pltpu

F32 = jnp.float32
BF16 = jnp.bfloat16

HEAD_DIM = 128
MOBA_BLOCK = 256
MOBA_TOPK = 3
PEER_TOPK = 16
PEER_HEADS = 8
PEER_N_KEYS = 128
NEG = -1e30
LN_EPS = 1e-5
RMS_EPS = 1e-6
ATTN_SCALE = HEAD_DIM ** -0.5

LANES = 128
SUBLANES = 8
VMEM_LIMIT_BYTES = 56 * 1024 * 1024

_NT = (((1,), (1,)), ((), ()))
_TN = (((0,), (0,)), ((), ()))


def _params(*semantics):
    return pltpu.CompilerParams(dimension_semantics=semantics, vmem_limit_bytes=VMEM_LIMIT_BYTES)


def _qkv_body(x_ref, w_ref, *refs):
    out_refs, xb_ref = refs[:-1], refs[-1]
    j = pl.program_id(1)

    @pl.when(j == 0)
    def _():
        xb_ref[...] = x_ref[...].astype(BF16)

    acc = jnp.dot(xb_ref[...], w_ref[...], preferred_element_type=F32)
    for k, o_ref in enumerate(out_refs):
        @pl.when(j == k)
        def _(o_ref=o_ref):
            o_ref[...] = acc


def qkv_proj(x, w_bf16, n_parts, tm):
    n, d = x.shape
    width = w_bf16.shape[1] // n_parts
    return pl.pallas_call(
        _qkv_body,
        grid=(n // tm, n_parts),
        in_specs=[pl.BlockSpec((tm, d), lambda i, j: (i, 0)),
                  pl.BlockSpec((d, width), lambda i, j: (0, j))],
        out_specs=[pl.BlockSpec((tm, width), lambda i, j: (i, 0))] * n_parts,
        out_shape=[jax.ShapeDtypeStruct((n, width), F32)] * n_parts,
        scratch_shapes=[pltpu.VMEM((tm, d), BF16)],
        compiler_params=_params("arbitrary", "arbitrary"),
        name="qkv_proj",
    )(x, w_bf16)


def _moba_prompt_body(slopes_ref, q_ref, k_ref, v_ref, o_ref, kb_ref, vt_ref, sel_ref):
    h = pl.program_id(1)
    i = pl.program_id(2)
    nb, blk, hd = kb_ref.shape
    s_len = nb * blk

    @pl.when(i == 0)
    def _():
        k = k_ref[...]
        v = v_ref[...]
        for n in range(nb):
            kb_ref[n] = k[n * blk:(n + 1) * blk, :].astype(BF16)
            vt_ref[n] = v[n * blk:(n + 1) * blk, :].T.astype(BF16)
        kbar = jnp.concatenate(
            [jnp.sum(k[n * blk:(n + 1) * blk, :], axis=0, keepdims=True) for n in range(nb)], axis=0
        ) * (1.0 / blk)
        gate = lax.dot_general(kbar.astype(BF16), q_ref[...].astype(BF16), _NT,
                               preferred_element_type=F32)
        own = lax.broadcasted_iota(jnp.int32, (nb, s_len), 1) // blk
        n_idx = lax.broadcasted_iota(jnp.int32, (nb, s_len), 0)
        rank = jnp.zeros((nb, s_len), jnp.int32)
        for m in range(nb):
            gm = gate[m:m + 1, :]
            beats = jnp.where(gm > gate, 1, jnp.where((gm == gate) & (m < n_idx), 1, 0))
            rank = rank + jnp.where(m < own, beats, 0)
        sel = jnp.where((n_idx < own) & (rank < MOBA_TOPK), 1.0, 0.0)
        for t in range(nb):
            sel_ref[t] = sel[:, t * blk:(t + 1) * blk]

    qi = q_ref[pl.ds(pl.multiple_of(i * blk, blk), blk), :].astype(BF16)
    slope = slopes_ref[h]
    t_pos = i * blk + lax.broadcasted_iota(jnp.int32, (1, blk), 1)
    r_iota = lax.broadcasted_iota(jnp.int32, (blk, 1), 0)

    def scores(j):
        st = lax.dot_general(kb_ref[j], qi, _NT, preferred_element_type=F32)
        dist = t_pos - (j * blk + r_iota)
        return st * ATTN_SCALE - slope * dist.astype(F32), dist

    def update(j, logit, carry):
        m, l, acc = carry
        m_new = jnp.maximum(m, jnp.max(logit, axis=0, keepdims=True))
        a = jnp.exp(m - m_new)
        p = jnp.exp(logit - m_new)
        l = a * l + jnp.sum(p, axis=0, keepdims=True)
        acc = a * acc + jnp.dot(vt_ref[j], p.astype(BF16), preferred_element_type=F32)
        return m_new, l, acc

    logit, dist = scores(i)
    logit = jnp.where(dist >= 0, logit, NEG)
    carry = update(i, logit, (jnp.full((1, blk), NEG, F32), jnp.zeros((1, blk), F32),
                              jnp.zeros((hd, blk), F32)))

    def body(j, carry):
        logit, _ = scores(j)
        valid = sel_ref[i, pl.ds(j, 1), :] > 0.5
        return update(j, jnp.where(valid, logit, NEG), carry)

    _, l, acc = lax.fori_loop(0, i, body, carry)
    o_ref[...] = (acc / l).T


def moba_prompt(q, k, v, slopes, n_seq, seq_len):
    n, width = q.shape
    n_heads = width // HEAD_DIM
    blk = MOBA_BLOCK
    nb = seq_len // blk
    seq_spec = pl.BlockSpec((seq_len, HEAD_DIM), lambda b, h, i: (b, h))
    return pl.pallas_call(
        _moba_prompt_body,
        grid=(n_seq, n_heads, nb),
        in_specs=[pl.BlockSpec(memory_space=pltpu.SMEM), seq_spec, seq_spec, seq_spec],
        out_specs=pl.BlockSpec((blk, HEAD_DIM), lambda b, h, i: (b * nb + i, h)),
        out_shape=jax.ShapeDtypeStruct((n, width), F32),
        scratch_shapes=[pltpu.VMEM((nb, blk, HEAD_DIM), BF16),
                        pltpu.VMEM((nb, HEAD_DIM, blk), BF16),
                        pltpu.VMEM((nb, nb, blk), F32)],
        compiler_params=_params("arbitrary", "arbitrary", "arbitrary"),
        name="moba_prompt",
    )(slopes, q, k, v)


def _softplus(z):
    return jnp.maximum(z, 0.0) + jnp.log(1.0 + jnp.exp(-jnp.abs(z)))


def _split_bf16(x):
    hi = x.astype(BF16)
    lo = (x - hi.astype(F32)).astype(BF16)
    return hi, lo


def _sb_prompt_body(q_ref, k_ref, v_ref, o_ref, kb_ref, vt_ref):
    i = pl.program_id(2)
    nb, blk, hd = kb_ref.shape

    @pl.when(i == 0)
    def _():
        k = k_ref[...]
        v = v_ref[...]
        for n in range(nb):
            kb_ref[n] = k[n * blk:(n + 1) * blk, :].astype(BF16)
            vt_ref[n] = v[n * blk:(n + 1) * blk, :].T.astype(BF16)

    qi = q_ref[pl.ds(pl.multiple_of(i * blk, blk), blk), :].astype(BF16)
    r_iota = lax.broadcasted_iota(jnp.int32, (blk, blk), 0)
    c_iota = lax.broadcasted_iota(jnp.int32, (blk, blk), 1)
    later = jnp.where(c_iota > r_iota, 1.0, 0.0).astype(BF16)

    def block(j, carry, causal):
        run, acc = carry
        z = lax.dot_general(kb_ref[j], qi, _NT, preferred_element_type=F32) * ATTN_SCALE
        sp = _softplus(z)
        log_keep = -sp
        log_beta = z - sp
        if causal is not None:
            log_keep = jnp.where(causal, log_keep, 0.0)
        hi, lo = _split_bf16(log_keep)
        after = (jnp.dot(later, hi, preferred_element_type=F32)
                 + jnp.dot(later, lo, preferred_element_type=F32) + run)
        a = jnp.exp(log_beta + after)
        if causal is not None:
            a = jnp.where(causal, a, 0.0)
        acc = acc + jnp.dot(vt_ref[j], a.astype(BF16), preferred_element_type=F32)
        run = run + jnp.sum(log_keep, axis=0, keepdims=True)
        return run, acc

    carry = block(i, (jnp.zeros((1, blk), F32), jnp.zeros((hd, blk), F32)), r_iota < c_iota)

    def body(jj, carry):
        return block(i - 1 - jj, carry, None)

    _, acc = lax.fori_loop(0, i, body, carry)
    o_ref[...] = acc.T


def sb_prompt(q, k, v, n_seq, seq_len, blk):
    n, width = q.shape
    n_heads = width // HEAD_DIM
    nb = seq_len // blk
    seq_spec = pl.BlockSpec((seq_len, HEAD_DIM), lambda b, h, i: (b, h))
    return pl.pallas_call(
        _sb_prompt_body,
        grid=(n_seq, n_heads, nb),
        in_specs=[seq_spec, seq_spec, seq_spec],
        out_specs=pl.BlockSpec((blk, HEAD_DIM), lambda b, h, i: (b * nb + i, h)),
        out_shape=jax.ShapeDtypeStruct((n, width), F32),
        scratch_shapes=[pltpu.VMEM((nb, blk, HEAD_DIM), BF16),
                        pltpu.VMEM((nb, HEAD_DIM, blk), BF16)],
        compiler_params=_params("arbitrary", "arbitrary", "arbitrary"),
        name="sb_prompt",
    )(q, k, v)


def _head_diag(o_full, n_heads, t):
    return jnp.concatenate(
        [o_full[h * t:(h + 1) * t, h * HEAD_DIM:(h + 1) * HEAD_DIM] for h in range(n_heads)], axis=0)


def _sample_body(pt_ref, slope_ref, qa_ref, ka_ref, va_ref, qb_ref, kb_ref, vb_ref,
                 pka_ref, pva_ref, pkb_ref, pvb_ref, oa_ref, ob_ref,
                 qa_bd, qb_bd, kbar_ref, m_all, l_all, o_all, run_ref, accb_ref, *, n_pages):
    s = pl.program_id(1)
    t_new, width = qa_ref.shape
    n_heads = width // HEAD_DIM
    rows = n_heads * t_new
    page = pka_ref.shape[0] // n_heads
    past_len = n_pages * page

    def load_page(ref):
        return jnp.concatenate([ref[pl.ds(h, page, stride=n_heads), :] for h in range(n_heads)], axis=1)

    pages_per_blk = MOBA_BLOCK // page

    lane = lax.broadcasted_iota(jnp.int32, (rows, page), 1)
    t_idx = lax.broadcasted_iota(jnp.int32, (rows, page), 0) % t_new
    r_iota = lax.broadcasted_iota(jnp.int32, (page, page), 0)
    c_iota = lax.broadcasted_iota(jnp.int32, (page, page), 1)
    later = jnp.where(r_iota > c_iota, 1.0, 0.0).astype(BF16)

    def moba_page(k_bf, v_bf, base, slot, new):
        sc = lax.dot_general(qa_bd[...], k_bf, _NT, preferred_element_type=F32)
        dist = (past_len + t_idx) - (base + lane)
        logit = sc * ATTN_SCALE - slope_ref[...] * dist.astype(F32)
        if new:
            logit = jnp.where(dist >= 0, logit, NEG)
        m = jnp.max(logit, axis=1, keepdims=True)
        p = jnp.exp(logit - m)
        l = jnp.sum(p, axis=1, keepdims=True)
        o = _head_diag(jnp.dot(p.astype(BF16), v_bf, preferred_element_type=F32), n_heads, t_new)
        m_all[...] = jnp.where(lane == slot, m, m_all[...])
        l_all[...] = jnp.where(lane == slot, l, l_all[...])
        o_all[slot] = o

    def sb_page(k_bf, v_bf, new):
        z = lax.dot_general(qb_bd[...], k_bf, _NT, preferred_element_type=F32) * ATTN_SCALE
        sp = _softplus(z)
        log_keep = -sp
        log_beta = z - sp
        if new:
            causal = lane < t_idx
            log_keep = jnp.where(causal, log_keep, 0.0)
        hi, lo = _split_bf16(log_keep)
        after = (jnp.dot(hi, later, preferred_element_type=F32)
                 + jnp.dot(lo, later, preferred_element_type=F32) + run_ref[...])
        a = jnp.exp(log_beta + after)
        if new:
            a = jnp.where(causal, a, 0.0)
        accb_ref[...] += _head_diag(jnp.dot(a.astype(BF16), v_bf, preferred_element_type=F32),
                                    n_heads, t_new)
        run_ref[...] += jnp.sum(log_keep, axis=1, keepdims=True)

    def new_page(ref):
        return jnp.concatenate([ref[...], jnp.zeros((page - t_new, width), F32)], axis=0).astype(BF16)

    @pl.when(s == 0)
    def _():
        head_of_row = lax.broadcasted_iota(jnp.int32, (rows, width), 0) // t_new
        head_of_col = lax.broadcasted_iota(jnp.int32, (rows, width), 1) // HEAD_DIM
        for q_ref, bd_ref in ((qa_ref, qa_bd), (qb_ref, qb_bd)):
            q_rep = jnp.concatenate([q_ref[...]] * n_heads, axis=0)
            bd_ref[...] = jnp.where(head_of_row == head_of_col, q_rep, 0.0).astype(BF16)
        kbar_ref[...] = jnp.zeros(kbar_ref.shape, F32)
        m_all[...] = jnp.full(m_all.shape, NEG, F32)
        l_all[...] = jnp.zeros(l_all.shape, F32)
        run_ref[...] = jnp.zeros(run_ref.shape, F32)
        accb_ref[...] = jnp.zeros(accb_ref.shape, F32)
        moba_page(new_page(ka_ref), new_page(va_ref), past_len, 0, True)
        sb_page(new_page(kb_ref), new_page(vb_ref), True)

    @pl.when(s > 0)
    def _():
        pg = n_pages - s
        ka = load_page(pka_ref)
        kbar_ref[pl.ds(pg // pages_per_blk, 1), :] += jnp.sum(ka, axis=0, keepdims=True)
        moba_page(ka.astype(BF16), load_page(pva_ref).astype(BF16), pg * page, pg + 1, False)
        sb_page(load_page(pkb_ref).astype(BF16), load_page(pvb_ref).astype(BF16), False)

    @pl.when(s == n_pages)
    def _():
        n_blk = past_len // MOBA_BLOCK
        kbar = (kbar_ref[...] * (1.0 / MOBA_BLOCK)).astype(BF16)
        gate = lax.dot_general(qa_bd[...], kbar, _NT, preferred_element_type=F32)
        rank = jnp.zeros((rows, page), jnp.int32)
        for m in range(n_blk):
            gm = gate[:, m:m + 1]
            rank = rank + jnp.where(gm > gate, 1, jnp.where((gm == gate) & (m < lane), 1, 0))
        chosen = jnp.where(rank < MOBA_TOPK, 1.0, 0.0)
        use = jnp.where(lane == 0, 1.0, 0.0)
        for n in range(n_blk):
            in_blk = (lane >= 1) & ((lane - 1) // pages_per_blk == n)
            use = jnp.where(in_blk, chosen[:, n:n + 1], use)
        used = use > 0.5
        m_top = jnp.max(jnp.where(used, m_all[...], NEG), axis=1, keepdims=True)
        w = jnp.where(used, jnp.exp(m_all[...] - m_top), 0.0)
        denom = jnp.sum(w * l_all[...], axis=1, keepdims=True)
        numer = jnp.zeros((rows, HEAD_DIM), F32)
        for slot in range(n_pages + 1):
            numer = numer + w[:, slot:slot + 1] * o_all[slot]
        oa = numer / denom
        ob = accb_ref[...]
        for h in range(n_heads):
            oa_ref[:, h * HEAD_DIM:(h + 1) * HEAD_DIM] = oa[h * t_new:(h + 1) * t_new, :]
            ob_ref[:, h * HEAD_DIM:(h + 1) * HEAD_DIM] = ob[h * t_new:(h + 1) * t_new, :]


def sample_attention(page_table, slope_rows, qa, ka, va, qb, kb, vb, pool_ka, pool_va, pool_kb, pool_vb,
                     t_new):
    n, width = qa.shape
    n_seq, n_pages = page_table.shape
    n_heads = width // HEAD_DIM
    page = pool_ka.shape[1] // n_heads
    rows = n_heads * t_new
    assert page == LANES and t_new == SUBLANES and n_pages + 1 <= LANES
    tok_spec = pl.BlockSpec((t_new, width), lambda b, s, pt: (b, 0))
    pool_spec = pl.BlockSpec((None, page * n_heads, HEAD_DIM),
                             lambda b, s, pt: (pt[b, n_pages - jnp.maximum(s, 1)], 0, 0))
    grid_spec = pltpu.PrefetchScalarGridSpec(
        num_scalar_prefetch=1,
        grid=(n_seq, n_pages + 1),
        in_specs=[pl.BlockSpec((rows, page), lambda b, s, pt: (0, 0))] + [tok_spec] * 6 + [pool_spec] * 4,
        out_specs=[tok_spec, tok_spec],
        scratch_shapes=[pltpu.VMEM((rows, width), BF16), pltpu.VMEM((rows, width), BF16),
                        pltpu.VMEM((page, width), F32),
                        pltpu.VMEM((rows, page), F32), pltpu.VMEM((rows, page), F32),
                        pltpu.VMEM((n_pages + 1, rows, HEAD_DIM), F32),
                        pltpu.VMEM((rows, page), F32), pltpu.VMEM((rows, HEAD_DIM), F32)],
    )
    return pl.pallas_call(
        functools.partial(_sample_body, n_pages=n_pages),
        grid_spec=grid_spec,
        out_shape=[jax.ShapeDtypeStruct((n, width), F32)] * 2,
        compiler_params=_params("arbitrary", "arbitrary"),
        name="sample_attention",
    )(page_table, slope_rows, qa, ka, va, qb, kb, vb, pool_ka, pool_va, pool_kb, pool_vb)


def _layer_norm(y, g, b):
    mu = jnp.mean(y, axis=-1, keepdims=True)
    yc = y - mu
    var = jnp.mean(yc * yc, axis=-1, keepdims=True)
    return yc * lax.rsqrt(var + LN_EPS) * g + b


def _rms_norm(y, g):
    return y * lax.rsqrt(jnp.mean(y * y, axis=-1, keepdims=True) + RMS_EPS) * g


def _mix_out_body(oa_ref, ob_ref, x_ref, w_ref, ga_ref, gb_ref, g1_ref, b1_ref, h_ref, ht_ref, *, alpha):
    o = jnp.concatenate([_rms_norm(oa_ref[...], ga_ref[...]), _rms_norm(ob_ref[...], gb_ref[...])],
                        axis=1).astype(BF16)
    y = alpha * x_ref[...] + jnp.dot(o, w_ref[...], preferred_element_type=F32)
    h = _layer_norm(y, g1_ref[...], b1_ref[...])
    h_ref[...] = h
    ht_ref[...] = h.T.astype(BF16)


def mix_out(oa, ob, x, w_out_bf16, ga, gb, g1, b1, alpha, tm):
    n, d = x.shape
    wa, wb = oa.shape[1], ob.shape[1]
    row = lambda width: pl.BlockSpec((1, width), lambda i: (0, 0))
    return pl.pallas_call(
        functools.partial(_mix_out_body, alpha=alpha),
        grid=(n // tm,),
        in_specs=[pl.BlockSpec((tm, wa), lambda i: (i, 0)), pl.BlockSpec((tm, wb), lambda i: (i, 0)),
                  pl.BlockSpec((tm, d), lambda i: (i, 0)), pl.BlockSpec((wa + wb, d), lambda i: (0, 0)),
                  row(wa), row(wb), row(d), row(d)],
        out_specs=[pl.BlockSpec((tm, d), lambda i: (i, 0)), pl.BlockSpec((d, tm), lambda i: (0, i))],
        out_shape=[jax.ShapeDtypeStruct((n, d), F32), jax.ShapeDtypeStruct((d, n), BF16)],
        compiler_params=_params("arbitrary"),
        name="mix_out",
    )(oa, ob, x, w_out_bf16, ga, gb, g1, b1)


def _kth_largest_rows(x, k):
    vals = []
    for it in range(k):
        m = jnp.max(x, axis=0, keepdims=True)
        vals.append(m)
        if it + 1 < k:
            x = jnp.where(x >= m, -jnp.inf, x)
    return vals


def _staircase(k):
    return [(a, b) for a in range(k) for b in range(k) if (a + 1) * (b + 1) <= k]


def _peer_select_body(h_ref, wq_ref, sk1_ref, sk2_ref, a1_ref, a2_ref, c1_ref, e2_ref, tau_ref, cand_ref):
    n_keys, half = sk1_ref.shape
    tm = h_ref.shape[0]
    qp = jnp.dot(h_ref[...].astype(BF16), wq_ref[...], preferred_element_type=F32)
    n_heads = qp.shape[1] // (2 * half)
    sk1 = sk1_ref[...].astype(BF16)
    sk2 = sk2_ref[...].astype(BF16)
    pairs = _staircase(PEER_TOPK)
    cand_ref[...] = jnp.full(cand_ref.shape, -jnp.inf, F32)
    for hh in range(n_heads):
        q1 = qp[:, hh * 2 * half: hh * 2 * half + half].astype(BF16)
        q2 = qp[:, hh * 2 * half + half: (hh + 1) * 2 * half].astype(BF16)
        s1 = lax.dot_general(sk1, q1, _NT, preferred_element_type=F32)
        s2 = lax.dot_general(sk2, q2, _NT, preferred_element_type=F32)
        v1 = _kth_largest_rows(s1, PEER_TOPK)
        v2 = _kth_largest_rows(s2, PEER_TOPK)
        for r, (a, b) in enumerate(pairs):
            cand_ref[pl.ds(r, 1), :] = v1[a] + v2[b]
        cand = cand_ref[...]
        tau = _kth_largest_rows(cand, PEER_TOPK)[-1]
        top = v1[0] + v2[0]
        z = jnp.sum(jnp.where(cand >= tau, jnp.exp(cand - top), 0.0), axis=0, keepdims=True)
        in1 = s1 >= v1[-1]
        in2 = s2 >= v2[-1]
        rows = pl.ds(hh * n_keys, n_keys)
        a1_ref[rows, :] = jnp.where(in1, s1, -jnp.inf)
        a2_ref[rows, :] = jnp.where(in2, s2, -jnp.inf)
        c1_ref[rows, :] = jnp.where(in1, jnp.exp(s1 - v1[0]), 0.0) / z
        e2_ref[rows, :] = jnp.where(in2, jnp.exp(s2 - v2[0]), 0.0)
        tau_ref[pl.ds(hh, 1), :] = tau


def peer_select(h, wq_bf16, sk1, sk2, tm):
    n, d = h.shape
    n_keys, half = sk1.shape
    n_heads = wq_bf16.shape[1] // (2 * half)
    n_cand = -(-len(_staircase(PEER_TOPK)) // SUBLANES) * SUBLANES
    sel_spec = pl.BlockSpec((n_heads * n_keys, tm), lambda i: (0, i))
    sel_shape = jax.ShapeDtypeStruct((n_heads * n_keys, n), F32)
    return pl.pallas_call(
        _peer_select_body,
        grid=(n // tm,),
        in_specs=[pl.BlockSpec((tm, d), lambda i: (i, 0)),
                  pl.BlockSpec(wq_bf16.shape, lambda i: (0, 0)),
                  pl.BlockSpec((n_keys, half), lambda i: (0, 0)),
                  pl.BlockSpec((n_keys, half), lambda i: (0, 0))],
        out_specs=[sel_spec] * 4 + [pl.BlockSpec((n_heads, tm), lambda i: (0, i))],
        out_shape=[sel_shape] * 4 + [jax.ShapeDtypeStruct((n_heads, n), F32)],
        scratch_shapes=[pltpu.VMEM((n_cand, tm), F32)],
        compiler_params=_params("arbitrary"),
        name="peer_select",
    )(h, wq_bf16, sk1, sk2)


def _gelu_tanh(x):
    return 0.5 * x * (1.0 + jnp.tanh(0.7978845608028654 * (x + 0.044715 * (x * x * x))))


def _peer_experts_body(ht_ref, u_ref, v_ref, a1_ref, a2_ref, c1_ref, e2_ref, tau_ref, h_ref, g2_ref, b2_ref,
                       y_ref, p_ref, *, alpha, n_keys):
    e = pl.program_id(1)
    te = u_ref.shape[0]
    n_heads = tau_ref.shape[0]

    @pl.when(e == 0)
    def _():
        y_ref[...] = alpha * h_ref[...]

    act = _gelu_tanh(jnp.dot(u_ref[...], ht_ref[...], preferred_element_type=F32))
    for ii in range(te // n_keys):
        i_glob = e * (te // n_keys) + ii
        w = jnp.zeros((n_keys, act.shape[1]), F32)
        for hh in range(n_heads):
            a1 = a1_ref[pl.ds(hh * n_keys + i_glob, 1), :]
            c1 = c1_ref[pl.ds(hh * n_keys + i_glob, 1), :]
            a2 = a2_ref[hh * n_keys:(hh + 1) * n_keys, :]
            e2 = e2_ref[hh * n_keys:(hh + 1) * n_keys, :]
            w = w + jnp.where(a1 + a2 >= tau_ref[hh:hh + 1, :], e2 * c1, 0.0)
        p_ref[ii * n_keys:(ii + 1) * n_keys, :] = (w * act[ii * n_keys:(ii + 1) * n_keys, :]).astype(BF16)
    y_ref[...] += lax.dot_general(p_ref[...], v_ref[...], _TN, preferred_element_type=F32)

    @pl.when(e == pl.num_programs(1) - 1)
    def _():
        y_ref[...] = _layer_norm(y_ref[...], g2_ref[...], b2_ref[...])


def peer_experts(h, ht, u_bf16, v_bf16, a1, a2, c1, e2, tau, g2, b2, alpha, tm, te):
    n, d = h.shape
    n_exp = u_bf16.shape[0]
    n_heads = tau.shape[0]
    n_keys = a1.shape[0] // n_heads
    sel_spec = pl.BlockSpec((n_heads * n_keys, tm), lambda i, e: (0, i))
    row = pl.BlockSpec((1, d), lambda i, e: (0, 0))
    return pl.pallas_call(
        functools.partial(_peer_experts_body, alpha=alpha, n_keys=n_keys),
        grid=(n // tm, n_exp // te),
        in_specs=[pl.BlockSpec((d, tm), lambda i, e: (0, i)),
                  pl.BlockSpec((te, d), lambda i, e: (e, 0)),
                  pl.BlockSpec((te, d), lambda i, e: (e, 0)),
                  sel_spec, sel_spec, sel_spec, sel_spec,
                  pl.BlockSpec((n_heads, tm), lambda i, e: (0, i)),
                  pl.BlockSpec((tm, d), lambda i, e: (i, 0)), row, row],
        out_specs=pl.BlockSpec((tm, d), lambda i, e: (i, 0)),
        out_shape=jax.ShapeDtypeStruct((n, d), F32),
        scratch_shapes=[pltpu.VMEM((te, tm), BF16)],
        compiler_params=_params("arbitrary", "arbitrary"),
        name="peer_experts",
    )(ht, u_bf16, v_bf16, a1, a2, c1, e2, tau, h, g2, b2)


def _finish(x, oa, ob, alpha, w_out, ga, gb, g1, b1, wq, sk1, sk2, u, v, g2, b2, tm_mix, tm_sel, tm_exp, te):
    h, ht = mix_out(oa, ob, x, w_out, ga, gb, g1, b1, alpha, tm_mix)
    a1, a2, c1, e2, tau = peer_select(h, wq, sk1, sk2, tm_sel)
    return peer_experts(h, ht, u, v, a1, a2, c1, e2, tau, g2, b2, alpha, tm_exp, te)


def kernel(x_prompt, x_sample, cache_k_a, cache_v_a, cache_k_b, cache_v_b, page_table, w_in, w_out, norm_a_g, norm_b_g, ln1_g, ln1_b, w_query, sub_keys, expert_u, expert_v, ln2_g, ln2_b):
    depth = w_in.shape[0]
    assert depth == 1, "the step is written for a single trunk layer"
    n_seq, seq_len, d = x_prompt.shape
    n_dec, t_new, _ = x_sample.shape
    h_a, h_b = cache_k_a.shape[3], cache_k_b.shape[3]
    assert h_a == h_b and cache_k_a.shape[4] == HEAD_DIM
    alpha = (2.0 * depth) ** 0.25
    slopes = jnp.exp2(-8.0 * jnp.arange(1, h_a + 1, dtype=F32) / h_a)
    slope_rows = jnp.broadcast_to(jnp.repeat(slopes, t_new)[:, None], (h_a * t_new, LANES))

    w_in_b = w_in[0].astype(BF16)
    w_out_b = w_out[0].astype(BF16)
    wq_b = w_query[0].reshape(d, -1).astype(BF16)
    u_b = expert_u[0].astype(BF16)
    v_b = expert_v[0].astype(BF16)
    row = lambda a: a[0].reshape(1, -1)
    fin = functools.partial(
        _finish, alpha=alpha, w_out=w_out_b, ga=row(norm_a_g), gb=row(norm_b_g), g1=row(ln1_g), b1=row(ln1_b),
        wq=wq_b, sk1=sub_keys[0, 0], sk2=sub_keys[0, 1], u=u_b, v=v_b, g2=row(ln2_g), b2=row(ln2_b))

    xp = x_prompt.reshape(n_seq * seq_len, d)
    qa, ka, va, qb, kb, vb = qkv_proj(xp, w_in_b, 6, 512)
    oa = moba_prompt(qa, ka, va, slopes, n_seq, seq_len)
    ob = sb_prompt(qb, kb, vb, n_seq, seq_len, 256)
    yp = fin(xp, oa, ob, tm_mix=256, tm_sel=256, tm_exp=512, te=512)
    prompt_kv = [a.reshape(depth, n_seq, seq_len, h_a, HEAD_DIM) for a in (ka, va, kb, vb)]

    xs = x_sample.reshape(n_dec * t_new, d)
    qa, ka, va, qb, kb, vb = qkv_proj(xs, w_in_b, 6, 512)
    pools = [c[0].reshape(c.shape[1], -1, HEAD_DIM) for c in (cache_k_a, cache_v_a, cache_k_b, cache_v_b)]
    oa, ob = sample_attention(page_table, slope_rows, qa, ka, va, qb, kb, vb, *pools, t_new=t_new)
    ys = fin(xs, oa, ob, tm_mix=256, tm_sel=256, tm_exp=512, te=512)
    sample_kv = [a.reshape(depth, n_dec, t_new, h_a, HEAD_DIM) for a in (ka, va, kb, vb)]

    return (yp.reshape(n_seq, seq_len, d), ys.reshape(n_dec, t_new, d), *prompt_kv, *sample_kv)
```

```python
import functools

import jax
import jax.numpy as jnp
from jax import lax
from jax.experimental import pallas as pl
from jax.experimental.pallas import tpu as pltpu

F32 = jnp.float32
BF16 = jnp.bfloat16

HEAD_DIM = 128
MOBA_BLOCK = 256
MOBA_TOPK = 3
PEER_TOPK = 16
PEER_HEADS = 8
PEER_N_KEYS = 128
NEG = -1e30
LN_EPS = 1e-5
RMS_EPS = 1e-6
ATTN_SCALE = HEAD_DIM ** -0.5

LANES = 128
SUBLANES = 8
VMEM_LIMIT_BYTES = 56 * 1024 * 1024

_NT = (((1,), (1,)), ((), ()))
_TN = (((0,), (0,)), ((), ()))


def _params(*semantics):
    return pltpu.CompilerParams(dimension_semantics=semantics, vmem_limit_bytes=VMEM_LIMIT_BYTES)


def _qkv_body(x_ref, w_ref, *refs):
    out_refs, xb_ref = refs[:-1], refs[-1]
    j = pl.program_id(1)

    @pl.when(j == 0)
    def _():
        xb_ref[...] = x_ref[...].astype(BF16)

    acc = jnp.dot(xb_ref[...], w_ref[...], preferred_element_type=F32)
    for k, o_ref in enumerate(out_refs):
        @pl.when(j == k)
        def _(o_ref=o_ref):
            o_ref[...] = acc


def qkv_proj(x, w_bf16, n_parts, tm):
    n, d = x.shape
    width = w_bf16.shape[1] // n_parts
    return pl.pallas_call(
        _qkv_body,
        grid=(n // tm, n_parts),
        in_specs=[pl.BlockSpec((tm, d), lambda i, j: (i, 0)),
                  pl.BlockSpec((d, width), lambda i, j: (0, j))],
        out_specs=[pl.BlockSpec((tm, width), lambda i, j: (i, 0))] * n_parts,
        out_shape=[jax.ShapeDtypeStruct((n, width), F32)] * n_parts,
        scratch_shapes=[pltpu.VMEM((tm, d), BF16)],
        compiler_params=_params("arbitrary", "arbitrary"),
        name="qkv_proj",
    )(x, w_bf16)


PROMPT_HEADS_PER_STEP = 2


def _moba_prompt_body(slopes_ref, q_ref, k_ref, v_ref, o_ref, kb_ref, vt_ref, sel_ref):
    i = pl.program_id(2)
    hpb, nb, blk, hd = kb_ref.shape
    s_len = nb * blk

    @pl.when(i == 0)
    def _():
        own = lax.broadcasted_iota(jnp.int32, (nb, s_len), 1) // blk
        n_idx = lax.broadcasted_iota(jnp.int32, (nb, s_len), 0)
        for hh in range(hpb):
            k = k_ref[:, hh * hd:(hh + 1) * hd]
            v = v_ref[:, hh * hd:(hh + 1) * hd]
            for n in range(nb):
                kb_ref[hh, n] = k[n * blk:(n + 1) * blk, :].astype(BF16)
                vt_ref[hh, n] = v[n * blk:(n + 1) * blk, :].T.astype(BF16)
            kbar = jnp.concatenate(
                [jnp.sum(k[n * blk:(n + 1) * blk, :], axis=0, keepdims=True) for n in range(nb)], axis=0
            ) * (1.0 / blk)
            gate = lax.dot_general(kbar.astype(BF16), q_ref[:, hh * hd:(hh + 1) * hd].astype(BF16), _NT,
                                   preferred_element_type=F32)
            rank = jnp.zeros((nb, s_len), jnp.int32)
            for m in range(nb):
                gm = gate[m:m + 1, :]
                beats = jnp.where(gm > gate, 1, jnp.where((gm == gate) & (m < n_idx), 1, 0))
                rank = rank + jnp.where(m < own, beats, 0)
            sel = jnp.where((n_idx < own) & (rank < MOBA_TOPK), 1.0, 0.0)
            for t in range(nb):
                sel_ref[hh, t] = sel[:, t * blk:(t + 1) * blk]

    rows = pl.ds(pl.multiple_of(i * blk, blk), blk)
    qi = [q_ref[rows, hh * hd:(hh + 1) * hd].astype(BF16) for hh in range(hpb)]
    slope = [slopes_ref[pl.program_id(1) * hpb + hh] for hh in range(hpb)]
    t_pos = i * blk + lax.broadcasted_iota(jnp.int32, (1, blk), 1)
    r_iota = lax.broadcasted_iota(jnp.int32, (blk, 1), 0)

    def scores(hh, j):
        st = lax.dot_general(kb_ref[hh, j], qi[hh], _NT, preferred_element_type=F32)
        dist = t_pos - (j * blk + r_iota)
        return st * ATTN_SCALE - slope[hh] * dist.astype(F32), dist

    def update(hh, j, logit, carry):
        m, l, acc = carry
        m_new = jnp.maximum(m, jnp.max(logit, axis=0, keepdims=True))
        a = jnp.exp(m - m_new)
        p = jnp.exp(logit - m_new)
        l = a * l + jnp.sum(p, axis=0, keepdims=True)
        acc = a * acc + jnp.dot(vt_ref[hh, j], p.astype(BF16), preferred_element_type=F32)
        return m_new, l, acc

    carry = []
    for hh in range(hpb):
        logit, dist = scores(hh, i)
        logit = jnp.where(dist >= 0, logit, NEG)
        carry.append(update(hh, i, logit, (jnp.full((1, blk), NEG, F32), jnp.zeros((1, blk), F32),
                                           jnp.zeros((hd, blk), F32))))

    def body(j, carry):
        out = []
        for hh in range(hpb):
            logit, _ = scores(hh, j)
            valid = sel_ref[hh, i, pl.ds(j, 1), :] > 0.5
            out.append(update(hh, j, jnp.where(valid, logit, NEG), carry[hh]))
        return tuple(out)

    carry = lax.fori_loop(0, i, body, tuple(carry))
    for hh in range(hpb):
        _, l, acc = carry[hh]
        o_ref[:, hh * hd:(hh + 1) * hd] = (acc / l).T


def moba_prompt(q, k, v, slopes, n_seq, seq_len):
    n, width = q.shape
    hpb = PROMPT_HEADS_PER_STEP
    n_heads = width // HEAD_DIM
    blk = MOBA_BLOCK
    nb = seq_len // blk
    seq_spec = pl.BlockSpec((seq_len, hpb * HEAD_DIM), lambda b, h, i: (b, h))
    return pl.pallas_call(
        _moba_prompt_body,
        grid=(n_seq, n_heads // hpb, nb),
        in_specs=[pl.BlockSpec(memory_space=pltpu.SMEM), seq_spec, seq_spec, seq_spec],
        out_specs=pl.BlockSpec((blk, hpb * HEAD_DIM), lambda b, h, i: (b * nb + i, h)),
        out_shape=jax.ShapeDtypeStruct((n, width), F32),
        scratch_shapes=[pltpu.VMEM((hpb, nb, blk, HEAD_DIM), BF16),
                        pltpu.VMEM((hpb, nb, HEAD_DIM, blk), BF16),
                        pltpu.VMEM((hpb, nb, nb, blk), F32)],
        compiler_params=_params("arbitrary", "arbitrary", "arbitrary"),
        name="moba_prompt",
    )(slopes, q, k, v)


def _softplus(z):
    return jnp.maximum(z, 0.0) + jnp.log(1.0 + jnp.exp(-jnp.abs(z)))


def _split_bf16(x):
    hi = x.astype(BF16)
    lo = (x - hi.astype(F32)).astype(BF16)
    return hi, lo


SB_DEAD = 110.0


def _sb_prompt_body(q_ref, k_ref, v_ref, o_ref, kb_ref, vt_ref):
    i = pl.program_id(2)
    hpb, nb, blk, hd = kb_ref.shape

    @pl.when(i == 0)
    def _():
        for hh in range(hpb):
            k = k_ref[:, hh * hd:(hh + 1) * hd]
            v = v_ref[:, hh * hd:(hh + 1) * hd]
            for n in range(nb):
                kb_ref[hh, n] = k[n * blk:(n + 1) * blk, :].astype(BF16)
                vt_ref[hh, n] = v[n * blk:(n + 1) * blk, :].T.astype(BF16)

    rows = pl.ds(pl.multiple_of(i * blk, blk), blk)
    qi = [q_ref[rows, hh * hd:(hh + 1) * hd].astype(BF16) for hh in range(hpb)]
    r_iota = lax.broadcasted_iota(jnp.int32, (blk, blk), 0)
    c_iota = lax.broadcasted_iota(jnp.int32, (blk, blk), 1)
    later = jnp.where(c_iota > r_iota, 1.0, 0.0).astype(BF16)

    def block(hh, j, run, acc, causal):
        z = lax.dot_general(kb_ref[hh, j], qi[hh], _NT, preferred_element_type=F32) * ATTN_SCALE
        sp = _softplus(z)
        log_keep = -sp
        log_beta = z - sp
        if causal is not None:
            log_keep = jnp.where(causal, log_keep, 0.0)
        hi, lo = _split_bf16(log_keep)
        after = (jnp.dot(later, hi, preferred_element_type=F32)
                 + jnp.dot(later, lo, preferred_element_type=F32) + run)
        a = jnp.exp(log_beta + after)
        if causal is not None:
            a = jnp.where(causal, a, 0.0)
        acc = acc + jnp.dot(vt_ref[hh, j], a.astype(BF16), preferred_element_type=F32)
        run = run + jnp.sum(log_keep, axis=0, keepdims=True)
        return run, acc

    state = [block(hh, i, jnp.zeros((1, blk), F32), jnp.zeros((hd, blk), F32), r_iota < c_iota)
             for hh in range(hpb)]
    runs, accs = tuple(s[0] for s in state), tuple(s[1] for s in state)

    def alive(c):
        jj, runs, _ = c
        top = runs[0]
        for r in runs[1:]:
            top = jnp.maximum(top, r)
        return (jj < i) & (jnp.max(top) > -SB_DEAD)

    def older_block(c):
        jj, runs, accs = c
        state = [block(hh, i - 1 - jj, runs[hh], accs[hh], None) for hh in range(hpb)]
        return jj + 1, tuple(s[0] for s in state), tuple(s[1] for s in state)

    _, _, accs = lax.while_loop(alive, older_block, (0, runs, accs))
    for hh in range(hpb):
        o_ref[:, hh * hd:(hh + 1) * hd] = accs[hh].T


def sb_prompt(q, k, v, n_seq, seq_len, blk):
    n, width = q.shape
    hpb = PROMPT_HEADS_PER_STEP
    n_heads = width // HEAD_DIM
    nb = seq_len // blk
    seq_spec = pl.BlockSpec((seq_len, hpb * HEAD_DIM), lambda b, h, i: (b, h))
    return pl.pallas_call(
        _sb_prompt_body,
        grid=(n_seq, n_heads // hpb, nb),
        in_specs=[seq_spec, seq_spec, seq_spec],
        out_specs=pl.BlockSpec((blk, hpb * HEAD_DIM), lambda b, h, i: (b * nb + i, h)),
        out_shape=jax.ShapeDtypeStruct((n, width), F32),
        scratch_shapes=[pltpu.VMEM((hpb, nb, blk, HEAD_DIM), BF16),
                        pltpu.VMEM((hpb, nb, HEAD_DIM, blk), BF16)],
        compiler_params=_params("arbitrary", "arbitrary", "arbitrary"),
        name="sb_prompt",
    )(q, k, v)


def _head_diag(o_full, n_heads, t):
    return jnp.concatenate(
        [o_full[h * t:(h + 1) * t, h * HEAD_DIM:(h + 1) * HEAD_DIM] for h in range(n_heads)], axis=0)


SB_CHUNK = 256
SAMPLE_PAGES_PER_STEP = 4


def _lanes(x, width):
    return x if width == LANES else jnp.concatenate([x] * (width // LANES), axis=1)


def _sample_body(pt_ref, slope_ref, qa_ref, ka_ref, va_ref, qb_ref, kb_ref, vb_ref, *refs, n_pages, pps):
    pools = [refs[k * pps:(k + 1) * pps] for k in range(4)]
    oa_ref, ob_ref, qa_bd, qb_bd, kbar_ref, m_all, l_all, o_all, run_ref, accb_ref = refs[4 * pps:]
    s = pl.program_id(1)
    t_new, width = qa_ref.shape
    n_heads = width // HEAD_DIM
    rows = n_heads * t_new
    page = pools[0][0].shape[0] // n_heads
    past_len = n_pages * page
    pages_per_blk = MOBA_BLOCK // page
    n_blk = past_len // MOBA_BLOCK
    slot_lane = lax.broadcasted_iota(jnp.int32, (rows, LANES), 1)

    def load_page(ref):
        return jnp.concatenate([ref[pl.ds(h, page, stride=n_heads), :] for h in range(n_heads)], axis=1)

    def load_pages(page_refs):
        return jnp.concatenate([load_page(r) for r in page_refs], axis=0)

    def key_iotas(w):
        lane = lax.broadcasted_iota(jnp.int32, (rows, w), 1)
        t_idx = lax.broadcasted_iota(jnp.int32, (rows, w), 0) % t_new
        return lane, t_idx

    def moba_keys(k_bf, v_bf, base, slot, new):
        w = k_bf.shape[0]
        lane, t_idx = key_iotas(w)
        sc = lax.dot_general(qa_bd[...], k_bf, _NT, preferred_element_type=F32)
        dist = (past_len + t_idx) - (base + lane)
        logit = sc * ATTN_SCALE - _lanes(slope_ref[...], w) * dist.astype(F32)
        if new:
            logit = jnp.where(dist >= 0, logit, NEG)
        m = jnp.max(logit, axis=1, keepdims=True)
        p = jnp.exp(logit - m)
        l = jnp.sum(p, axis=1, keepdims=True)
        o = _head_diag(jnp.dot(p.astype(BF16), v_bf, preferred_element_type=F32), n_heads, t_new)
        m_all[...] = jnp.where(slot_lane == slot, m, m_all[...])
        l_all[...] = jnp.where(slot_lane == slot, l, l_all[...])
        o_all[slot] = o

    def sb_keys(k_bf, v_bf, new):
        w = k_bf.shape[0]
        lane, t_idx = key_iotas(w)
        z = lax.dot_general(qb_bd[...], k_bf, _NT, preferred_element_type=F32) * ATTN_SCALE
        sp = _softplus(z)
        log_keep = -sp
        log_beta = z - sp
        if new:
            causal = lane < t_idx
            log_keep = jnp.where(causal, log_keep, 0.0)
        ch = min(w, SB_CHUNK)
        r_iota = lax.broadcasted_iota(jnp.int32, (ch, ch), 0)
        c_iota = lax.broadcasted_iota(jnp.int32, (ch, ch), 1)
        later = jnp.where(r_iota > c_iota, 1.0, 0.0).astype(BF16)
        run = run_ref[...]
        after = [None] * (w // ch)
        for c in reversed(range(w // ch)):
            lk = log_keep[:, c * ch:(c + 1) * ch]
            hi, lo = _split_bf16(lk)
            after[c] = (jnp.dot(hi, later, preferred_element_type=F32)
                        + jnp.dot(lo, later, preferred_element_type=F32) + _lanes(run, ch))
            run = run + jnp.sum(lk, axis=1, keepdims=True)
        a = jnp.exp(log_beta + jnp.concatenate(after, axis=1))
        if new:
            a = jnp.where(causal, a, 0.0)
        accb_ref[...] += _head_diag(jnp.dot(a.astype(BF16), v_bf, preferred_element_type=F32),
                                    n_heads, t_new)
        run_ref[...] = run

    def new_page(ref):
        return jnp.concatenate([ref[...], jnp.zeros((page - t_new, width), F32)], axis=0).astype(BF16)

    @pl.when(s == 0)
    def _():
        head_of_row = lax.broadcasted_iota(jnp.int32, (rows, width), 0) // t_new
        head_of_col = lax.broadcasted_iota(jnp.int32, (rows, width), 1) // HEAD_DIM
        for q_ref, bd_ref in ((qa_ref, qa_bd), (qb_ref, qb_bd)):
            q_rep = jnp.concatenate([q_ref[...]] * n_heads, axis=0)
            bd_ref[...] = jnp.where(head_of_row == head_of_col, q_rep, 0.0).astype(BF16)
        kbar_ref[...] = jnp.zeros(kbar_ref.shape, F32)
        m_all[...] = jnp.full(m_all.shape, NEG, F32)
        l_all[...] = jnp.zeros(l_all.shape, F32)
        run_ref[...] = jnp.zeros(run_ref.shape, F32)
        accb_ref[...] = jnp.zeros(accb_ref.shape, F32)
        moba_keys(new_page(ka_ref), new_page(va_ref), past_len, n_blk, True)
        sb_keys(new_page(kb_ref), new_page(vb_ref), True)

    @pl.when(s > 0)
    def _():
        pg0 = n_pages - s * pps
        for c in range(pps // pages_per_blk):
            in_blk = slice(c * pages_per_blk, (c + 1) * pages_per_blk)
            blk = pg0 // pages_per_blk + c
            ka = load_pages(pools[0][in_blk])
            kbar_ref[pl.ds(blk, 1), :] = jnp.sum(ka, axis=0, keepdims=True) * (1.0 / MOBA_BLOCK)
            moba_keys(ka.astype(BF16), load_pages(pools[1][in_blk]).astype(BF16), blk * MOBA_BLOCK, blk, False)
        sb_keys(load_pages(pools[2]).astype(BF16), load_pages(pools[3]).astype(BF16), False)

    @pl.when(s == n_pages // pps)
    def _():
        gate = lax.dot_general(qa_bd[...], kbar_ref[...].astype(BF16), _NT,
                               preferred_element_type=F32)
        rank = jnp.zeros((rows, LANES), jnp.int32)
        for m in range(n_blk):
            gm = gate[:, m:m + 1]
            rank = rank + jnp.where(gm > gate, 1, jnp.where((gm == gate) & (m < slot_lane), 1, 0))
        used = ((slot_lane < n_blk) & (rank < MOBA_TOPK)) | (slot_lane == n_blk)
        m_top = jnp.max(jnp.where(used, m_all[...], NEG), axis=1, keepdims=True)
        w = jnp.where(used, jnp.exp(m_all[...] - m_top), 0.0)
        denom = jnp.sum(w * l_all[...], axis=1, keepdims=True)
        numer = jnp.zeros((rows, HEAD_DIM), F32)
        for slot in range(n_blk + 1):
            numer = numer + w[:, slot:slot + 1] * o_all[slot]
        oa = numer / denom
        ob = accb_ref[...]
        for h in range(n_heads):
            oa_ref[:, h * HEAD_DIM:(h + 1) * HEAD_DIM] = oa[h * t_new:(h + 1) * t_new, :]
            ob_ref[:, h * HEAD_DIM:(h + 1) * HEAD_DIM] = ob[h * t_new:(h + 1) * t_new, :]


def sample_attention(page_table, slope_rows, qa, ka, va, qb, kb, vb, pool_ka, pool_va, pool_kb, pool_vb,
                     t_new):
    n, width = qa.shape
    n_seq, n_pages = page_table.shape
    n_heads = width // HEAD_DIM
    page = pool_ka.shape[1] // n_heads
    rows = n_heads * t_new
    past_len = n_pages * page
    n_blk = past_len // MOBA_BLOCK
    pps = SAMPLE_PAGES_PER_STEP
    assert page == LANES and t_new == SUBLANES and n_blk < LANES
    assert n_pages % pps == 0 and (pps * page) % MOBA_BLOCK == 0
    tok_spec = pl.BlockSpec((t_new, width), lambda b, s, pt: (b, 0))

    def pool_spec(k):
        return pl.BlockSpec((None, page * n_heads, HEAD_DIM),
                            lambda b, s, pt: (pt[b, n_pages - jnp.maximum(s, 1) * pps + k], 0, 0))

    pool_specs = [pool_spec(k) for k in range(pps)]
    grid_spec = pltpu.PrefetchScalarGridSpec(
        num_scalar_prefetch=1,
        grid=(n_seq, n_pages // pps + 1),
        in_specs=[pl.BlockSpec((rows, LANES), lambda b, s, pt: (0, 0))] + [tok_spec] * 6 + pool_specs * 4,
        out_specs=[tok_spec, tok_spec],
        scratch_shapes=[pltpu.VMEM((rows, width), BF16), pltpu.VMEM((rows, width), BF16),
                        pltpu.VMEM((LANES, width), F32),
                        pltpu.VMEM((rows, LANES), F32), pltpu.VMEM((rows, LANES), F32),
                        pltpu.VMEM((n_blk + 1, rows, HEAD_DIM), F32),
                        pltpu.VMEM((rows, LANES), F32), pltpu.VMEM((rows, HEAD_DIM), F32)],
    )
    pool_args = [p for pool in (pool_ka, pool_va, pool_kb, pool_vb) for p in [pool] * pps]
    return pl.pallas_call(
        functools.partial(_sample_body, n_pages=n_pages, pps=pps),
        grid_spec=grid_spec,
        out_shape=[jax.ShapeDtypeStruct((n, width), F32)] * 2,
        compiler_params=_params("arbitrary", "arbitrary"),
        name="sample_attention",
    )(page_table, slope_rows, qa, ka, va, qb, kb, vb, *pool_args)


def _layer_norm(y, g, b):
    mu = jnp.mean(y, axis=-1, keepdims=True)
    yc = y - mu
    var = jnp.mean(yc * yc, axis=-1, keepdims=True)
    return yc * lax.rsqrt(var + LN_EPS) * g + b


def _rms_norm(y, g):
    return y * lax.rsqrt(jnp.mean(y * y, axis=-1, keepdims=True) + RMS_EPS) * g


def _mix_out_body(oa_ref, ob_ref, x_ref, w_ref, ga_ref, gb_ref, g1_ref, b1_ref, h_ref, ht_ref, *, alpha):
    o = jnp.concatenate([_rms_norm(oa_ref[...], ga_ref[...]), _rms_norm(ob_ref[...], gb_ref[...])],
                        axis=1).astype(BF16)
    y = alpha * x_ref[...] + jnp.dot(o, w_ref[...], preferred_element_type=F32)
    h = _layer_norm(y, g1_ref[...], b1_ref[...])
    h_ref[...] = h
    ht_ref[...] = h.T.astype(BF16)


def mix_out(oa, ob, x, w_out_bf16, ga, gb, g1, b1, alpha, tm):
    n, d = x.shape
    wa, wb = oa.shape[1], ob.shape[1]
    row = lambda width: pl.BlockSpec((1, width), lambda i: (0, 0))
    return pl.pallas_call(
        functools.partial(_mix_out_body, alpha=alpha),
        grid=(n // tm,),
        in_specs=[pl.BlockSpec((tm, wa), lambda i: (i, 0)), pl.BlockSpec((tm, wb), lambda i: (i, 0)),
                  pl.BlockSpec((tm, d), lambda i: (i, 0)), pl.BlockSpec((wa + wb, d), lambda i: (0, 0)),
                  row(wa), row(wb), row(d), row(d)],
        out_specs=[pl.BlockSpec((tm, d), lambda i: (i, 0)), pl.BlockSpec((d, tm), lambda i: (0, i))],
        out_shape=[jax.ShapeDtypeStruct((n, d), F32), jax.ShapeDtypeStruct((d, n), BF16)],
        compiler_params=_params("arbitrary"),
        name="mix_out",
    )(oa, ob, x, w_out_bf16, ga, gb, g1, b1)


def _kth_largest_rows(x, k):
    vals = []
    for it in range(k):
        m = jnp.max(x, axis=0, keepdims=True)
        vals.append(m)
        if it + 1 < k:
            x = jnp.where(x >= m, -jnp.inf, x)
    return vals


def _top_ranks(x, k):
    vals = []
    rank = jnp.full(x.shape, float(k), F32)
    for it in range(k):
        m = jnp.max(x, axis=0, keepdims=True)
        vals.append(m)
        hit = x >= m
        rank = jnp.where(hit, float(it), rank)
        x = jnp.where(hit, -jnp.inf, x)
    return vals, rank


def _staircase(k):
    return [[(a, b) for b in range(k) if (a + 1) * (b + 1) <= k] for a in range(k)]


def _peer_select_body(h_ref, wq_ref, sk1_ref, sk2_ref, r2_ref, e2_ref, reach_ref, c1_ref, cand_ref):
    n_keys, half = sk1_ref.shape
    qp = jnp.dot(h_ref[...].astype(BF16), wq_ref[...], preferred_element_type=F32)
    n_heads = qp.shape[1] // (2 * half)
    sk1 = sk1_ref[...].astype(BF16)
    sk2 = sk2_ref[...].astype(BF16)
    stairs = _staircase(PEER_TOPK)
    cand_ref[...] = jnp.full(cand_ref.shape, -jnp.inf, F32)
    for hh in range(n_heads):
        q1 = qp[:, hh * 2 * half: hh * 2 * half + half].astype(BF16)
        q2 = qp[:, hh * 2 * half + half: (hh + 1) * 2 * half].astype(BF16)
        s1 = lax.dot_general(sk1, q1, _NT, preferred_element_type=F32)
        s2 = lax.dot_general(sk2, q2, _NT, preferred_element_type=F32)
        v1, r1 = _top_ranks(s1, PEER_TOPK)
        v2, r2 = _top_ranks(s2, PEER_TOPK)
        r = 0
        for a, row in enumerate(stairs):
            for _, b in row:
                cand_ref[pl.ds(r, 1), :] = v1[a] + v2[b]
                r += 1
        cand = cand_ref[...]
        tau = _kth_largest_rows(cand, PEER_TOPK)[-1]
        keep = cand >= tau
        z = jnp.sum(jnp.where(keep, jnp.exp(cand - (v1[0] + v2[0])), 0.0), axis=0, keepdims=True)
        kept = jnp.where(keep, 1.0, 0.0)
        reach = jnp.zeros(s1.shape, F32)
        r = 0
        for a, row in enumerate(stairs):
            n_kept = jnp.sum(kept[r:r + len(row), :], axis=0, keepdims=True)
            reach = jnp.where(r1 == float(a), n_kept, reach)
            r += len(row)
        rows = pl.ds(hh * n_keys, n_keys)
        r2_ref[rows, :] = r2.astype(BF16)
        e2_ref[rows, :] = jnp.exp(s2 - v2[0]).astype(BF16)
        reach_ref[rows, :] = reach
        c1_ref[rows, :] = jnp.exp(s1 - v1[0]) / z


def peer_select(h, wq_bf16, sk1, sk2, tm):
    n, d = h.shape
    n_keys, half = sk1.shape
    n_heads = wq_bf16.shape[1] // (2 * half)
    n_cand = sum(len(row) for row in _staircase(PEER_TOPK))
    n_cand = -(-n_cand // SUBLANES) * SUBLANES
    sel_spec = pl.BlockSpec((n_heads * n_keys, tm), lambda i: (0, i))
    sel_shape = lambda dtype: jax.ShapeDtypeStruct((n_heads * n_keys, n), dtype)
    return pl.pallas_call(
        _peer_select_body,
        grid=(n // tm,),
        in_specs=[pl.BlockSpec((tm, d), lambda i: (i, 0)),
                  pl.BlockSpec(wq_bf16.shape, lambda i: (0, 0)),
                  pl.BlockSpec((n_keys, half), lambda i: (0, 0)),
                  pl.BlockSpec((n_keys, half), lambda i: (0, 0))],
        out_specs=[sel_spec] * 4,
        out_shape=[sel_shape(BF16), sel_shape(BF16), sel_shape(F32), sel_shape(F32)],
        scratch_shapes=[pltpu.VMEM((n_cand, tm), F32)],
        compiler_params=_params("arbitrary"),
        name="peer_select",
    )(h, wq_bf16, sk1, sk2)


def _gelu_tanh(x):
    return 0.5 * x * (1.0 + jnp.tanh(0.7978845608028654 * (x + 0.044715 * (x * x * x))))


BF16_ROWS = 2 * SUBLANES
ACT_ROWS = 256
GATE_LANES = 256
LN_ROWS = 32


def _peer_experts_body(ht_ref, u_ref, v_ref, r2_ref, e2_ref, reach_ref, c1_ref, h_ref, g2_ref, b2_ref,
                       y_ref, p_ref, *, alpha, n_keys):
    e = pl.program_id(1)
    te = u_ref.shape[0]
    tm = ht_ref.shape[1]
    n_heads = r2_ref.shape[0] // n_keys
    n_grp = n_keys // BF16_ROWS

    @pl.when(e == 0)
    def _():
        y_ref[...] = alpha * h_ref[...]

    def expert_acts(c):
        return jnp.dot(u_ref[c * ACT_ROWS:(c + 1) * ACT_ROWS, :], ht_ref[...],
                       preferred_element_type=F32)

    def gate_weights(c):
        for ii, t0 in [(ii, t0) for ii in range(ACT_ROWS // n_keys) for t0 in range(0, tm, GATE_LANES)]:
            i_glob = (e * te + c * ACT_ROWS) // n_keys + ii
            cols = slice(t0, t0 + GATE_LANES)
            w = [None] * n_grp
            for hh in range(n_heads):
                row = pl.ds(hh * n_keys + i_glob, 1)
                reach = jnp.broadcast_to(reach_ref[row, cols], (BF16_ROWS, GATE_LANES)).astype(BF16)
                c1 = jnp.broadcast_to(c1_ref[row, cols], (BF16_ROWS, GATE_LANES)).astype(BF16)
                for g in range(n_grp):
                    rows = slice(hh * n_keys + g * BF16_ROWS, hh * n_keys + (g + 1) * BF16_ROWS)
                    gate = jnp.where(r2_ref[rows, cols] < reach, e2_ref[rows, cols] * c1, 0.0)
                    w[g] = gate if w[g] is None else w[g] + gate
            for g in range(n_grp):
                p_ref[pl.ds(c * ACT_ROWS + ii * n_keys + g * BF16_ROWS, BF16_ROWS), cols] = w[g]

    def apply_acts(c, act):
        for lo in range(0, ACT_ROWS, BF16_ROWS):
            rows = pl.ds(c * ACT_ROWS + lo, BF16_ROWS)
            p_ref[rows, :] = p_ref[rows, :] * _gelu_tanh(act[lo:lo + BF16_ROWS, :]).astype(BF16)

    act_prev = None
    for c in range(te // ACT_ROWS):
        act = expert_acts(c)
        gate_weights(c)
        if act_prev is not None:
            apply_acts(c - 1, act_prev)
        act_prev = act
    apply_acts(te // ACT_ROWS - 1, act_prev)
    y_ref[...] += lax.dot_general(p_ref[...], v_ref[...], _TN, preferred_element_type=F32)

    @pl.when(e == pl.num_programs(1) - 1)
    def _():
        def norm_rows(c, carry):
            rows = pl.ds(pl.multiple_of(c * LN_ROWS, LN_ROWS), LN_ROWS)
            y_ref[rows, :] = _layer_norm(y_ref[rows, :], g2_ref[...], b2_ref[...])
            return carry
        lax.fori_loop(0, tm // LN_ROWS, norm_rows, 0)


def peer_experts(h, ht, u_bf16, v_bf16, r2, e2, reach, c1, g2, b2, alpha, tm, te):
    n, d = h.shape
    n_exp = u_bf16.shape[0]
    n_keys = PEER_N_KEYS
    once = dict(pipeline_mode=pl.Buffered(1))
    sel_spec = pl.BlockSpec((r2.shape[0], tm), lambda i, e: (0, i), **once)
    row = pl.BlockSpec((1, d), lambda i, e: (0, 0))
    return pl.pallas_call(
        functools.partial(_peer_experts_body, alpha=alpha, n_keys=n_keys),
        grid=(n // tm, n_exp // te),
        in_specs=[pl.BlockSpec((d, tm), lambda i, e: (0, i), **once),
                  pl.BlockSpec((te, d), lambda i, e: (e, 0)),
                  pl.BlockSpec((te, d), lambda i, e: (e, 0)),
                  sel_spec, sel_spec, sel_spec, sel_spec,
                  pl.BlockSpec((tm, d), lambda i, e: (i, 0), **once), row, row],
        out_specs=pl.BlockSpec((tm, d), lambda i, e: (i, 0)),
        out_shape=jax.ShapeDtypeStruct((n, d), F32),
        scratch_shapes=[pltpu.VMEM((te, tm), BF16)],
        compiler_params=_params("arbitrary", "arbitrary"),
        name="peer_experts",
    )(ht, u_bf16, v_bf16, r2, e2, reach, c1, h, g2, b2)


def _finish(x, oa, ob, alpha, w_out, ga, gb, g1, b1, wq, sk1, sk2, u, v, g2, b2, tm_mix, tm_sel, tm_exp, te):
    h, ht = mix_out(oa, ob, x, w_out, ga, gb, g1, b1, alpha, tm_mix)
    r2, e2, reach, c1 = peer_select(h, wq, sk1, sk2, tm_sel)
    return peer_experts(h, ht, u, v, r2, e2, reach, c1, g2, b2, alpha, tm_exp, te)


def kernel(x_prompt, x_sample, cache_k_a, cache_v_a, cache_k_b, cache_v_b, page_table, w_in, w_out, norm_a_g, norm_b_g, ln1_g, ln1_b, w_query, sub_keys, expert_u, expert_v, ln2_g, ln2_b):
    depth = w_in.shape[0]
    assert depth == 1, "the step is written for a single trunk layer"
    n_seq, seq_len, d = x_prompt.shape
    n_dec, t_new, _ = x_sample.shape
    h_a, h_b = cache_k_a.shape[3], cache_k_b.shape[3]
    assert h_a == h_b and cache_k_a.shape[4] == HEAD_DIM
    alpha = (2.0 * depth) ** 0.25
    slopes = jnp.exp2(-8.0 * jnp.arange(1, h_a + 1, dtype=F32) / h_a)
    slope_rows = jnp.broadcast_to(jnp.repeat(slopes, t_new)[:, None], (h_a * t_new, LANES))

    w_in_b = w_in[0].astype(BF16)
    w_out_b = w_out[0].astype(BF16)
    wq_b = w_query[0].reshape(d, -1).astype(BF16)
    u_b = expert_u[0].astype(BF16)
    v_b = expert_v[0].astype(BF16)
    row = lambda a: a[0].reshape(1, -1)
    fin = functools.partial(
        _finish, alpha=alpha, w_out=w_out_b, ga=row(norm_a_g), gb=row(norm_b_g), g1=row(ln1_g), b1=row(ln1_b),
        wq=wq_b, sk1=sub_keys[0, 0], sk2=sub_keys[0, 1], u=u_b, v=v_b, g2=row(ln2_g), b2=row(ln2_b))

    xp = x_prompt.reshape(n_seq * seq_len, d)
    qa, ka, va, qb, kb, vb = qkv_proj(xp, w_in_b, 6, 512)
    oa = moba_prompt(qa, ka, va, slopes, n_seq, seq_len)
    ob = sb_prompt(qb, kb, vb, n_seq, seq_len, 256)
    yp = fin(xp, oa, ob, tm_mix=256, tm_sel=128, tm_exp=512, te=1024)
    prompt_kv = [a.reshape(depth, n_seq, seq_len, h_a, HEAD_DIM) for a in (ka, va, kb, vb)]

    xs = x_sample.reshape(n_dec * t_new, d)
    qa, ka, va, qb, kb, vb = qkv_proj(xs, w_in_b, 6, 512)
    pools = [c[0].reshape(c.shape[1], -1, HEAD_DIM) for c in (cache_k_a, cache_v_a, cache_k_b, cache_v_b)]
    oa, ob = sample_attention(page_table, slope_rows, qa, ka, va, qb, kb, vb, *pools, t_new=t_new)
    ys = fin(xs, oa, ob, tm_mix=256, tm_sel=128, tm_exp=512, te=1024)
    sample_kv = [a.reshape(depth, n_dec, t_new, h_a, HEAD_DIM) for a in (ka, va, kb, vb)]

    return (yp.reshape(n_seq, seq_len, d), ys.reshape(n_dec, t_new, d), *prompt_kv, *sample_kv)
```
